```python
import jax, jax.numpy as jnp
from jax import lax
import numpy as np

D_MODEL = 1024
BATCH = 1
SEQ = 16384
DEPTH = 1

GRID_W = 64
NA_W = D_MODEL // 2
NA_HEAD_DIM = 64
NA_HEADS = NA_W // NA_HEAD_DIM
NA_KH = 8
NA_KW = 16
ML_W = D_MODEL // 2
ML_HEADS = 4
ML_HEAD_DIM = ML_W // ML_HEADS
MIX_W = NA_W + ML_W
ML_CHUNK = 64
CONV_W = 5
EPS = 1e-6
IN_SPLITS = (NA_W,) * 4 + (ML_W,) * 5 + (ML_HEADS,) * 4
IN_W = 4 * NA_W + 5 * ML_W + 4 * ML_HEADS

kernel_name = 'hybrid_natten2d_bimlstm_block'


def rmsnorm(x, w):
    x32 = x.astype(jnp.float32)
    return x32 * lax.rsqrt(jnp.mean(x32 * x32, axis=-1, keepdims=True) + EPS) * w


def centred_short_conv(u, w, b):
    T = u.shape[1]
    pad = CONV_W // 2
    up = jnp.pad(u, ((0, 0), (pad, pad), (0, 0)))
    out = up[:, 0:T] * w[0]
    for j in range(1, CONV_W):
        out = out + up[:, j:j + T] * w[j]
    return out + b


def neighbourhood_attention(q, k, v, rpb):
    B, T, _ = q.shape
    rows = T // GRID_W
    kh = min(NA_KH, rows)

    def to_grid(t):
        return t.reshape(B, rows, GRID_W, NA_HEADS, NA_HEAD_DIM).transpose(0, 3, 1, 2, 4)

    qg = to_grid(q) * (NA_HEAD_DIM ** -0.5)
    kgrid, vgrid = to_grid(k), to_grid(v)
    r = jnp.arange(rows)
    row_idx = jnp.clip(r - kh // 2, 0, rows - kh)[:, None] + jnp.arange(kh)[None, :]
    kg = kgrid[:, :, row_idx]
    vg = vgrid[:, :, row_idx]
    col = jnp.arange(GRID_W)
    col_start = jnp.clip(col - NA_KW // 2, 0, GRID_W - NA_KW)
    col_mask = (col[None, :] >= col_start[:, None]) & (col[None, :] < col_start[:, None] + NA_KW)
    dy = row_idx - r[:, None] + NA_KH - 1
    dx = jnp.clip(col[None, :] - col[:, None], -(NA_KW - 1), NA_KW - 1) + NA_KW - 1
    bias = rpb[:, dy[:, None, :, None], dx[None, :, None, :]]
    s = jnp.einsum('bhrcd,bhrjkd->bhrcjk', qg, kg).astype(jnp.float32) + bias[None]
    s = jnp.where(col_mask[:, None, :], s, -jnp.inf)
    p = jax.nn.softmax(s.reshape(B, NA_HEADS, rows, GRID_W, kh * GRID_W), axis=-1).reshape(s.shape)
    o = jnp.einsum('bhrcjk,bhrjkd->bhrcd', p, vg)
    return o.transpose(0, 2, 3, 1, 4).reshape(B, T, NA_W)


def mlstm_chunkwise(q, k, v, i_pre, f_pre):
    B, H, T, d = q.shape
    L = ML_CHUNK
    nc = T // L
    q = q.reshape(B, H, nc, L, d)
    k = (k * (d ** -0.5)).reshape(B, H, nc, L, d)
    v = v.reshape(B, H, nc, L, d)
    i_pre = i_pre.reshape(B, H, nc, L)
    b = jnp.cumsum(jax.nn.log_sigmoid(f_pre).reshape(B, H, nc, L), axis=-1)
    b_last = b[..., -1]
    a = b_last[..., None] - b + i_pre

    def step(carry, inp):
        C, n, m = carry
        k_c, v_c, a_c, bl_c = inp
        m_new = jnp.maximum(bl_c + m, jnp.max(a_c, axis=-1))
        decay = jnp.exp(bl_c + m - m_new)
        w = jnp.exp(a_c - m_new[..., None])
        C_new = decay[..., None, None] * C + jnp.einsum('bhs,bhse,bhsd->bhed', w, v_c, k_c)
        n_new = decay[..., None] * n + jnp.einsum('bhs,bhsd->bhd', w, k_c)
        return (C_new, n_new, m_new), (C, n, m)

    init = (jnp.zeros((B, H, d, d), q.dtype), jnp.zeros((B, H, d), q.dtype), jnp.zeros((B, H), q.dtype))
    xs = (jnp.moveaxis(k, 2, 0), jnp.moveaxis(v, 2, 0), jnp.moveaxis(a, 2, 0), jnp.moveaxis(b_last, 2, 0))
    _, (C_prev, n_prev, m_prev) = lax.scan(step, init, xs)
    C_prev = jnp.moveaxis(C_prev, 0, 2)
    n_prev = jnp.moveaxis(n_prev, 0, 2)
    m_prev = jnp.moveaxis(m_prev, 0, 2)

    lower = jnp.tril(jnp.ones((L, L), dtype=bool))
    Dlog = jnp.where(lower, b[..., :, None] - b[..., None, :] + i_pre[..., None, :], -jnp.inf)
    m_inter = b + m_prev[..., None]
    m_t = jnp.maximum(m_inter, jnp.max(Dlog, axis=-1))
    S = jnp.einsum('bhntd,bhnsd->bhnts', q, k) * jnp.exp(Dlog - m_t[..., None])
    inter = jnp.exp(m_inter - m_t)
    num = jnp.einsum('bhnts,bhnse->bhnte', S, v) + inter[..., None] * jnp.einsum('bhned,bhntd->bhnte', C_prev, q)
    den = jnp.sum(S, axis=-1) + inter * jnp.einsum('bhnd,bhntd->bhnt', n_prev, q)
    h = num / jnp.maximum(jnp.abs(den), jnp.exp(-m_t))[..., None]
    return h.reshape(B, H, T, d)


def hybrid_mixer(h, w_in, b_in, conv_w, conv_b, rpb, ml_norm_w, w_out):
    B, T, _ = h.shape
    proj = h @ w_in + b_in
    split_idx = np.cumsum(IN_SPLITS)[:-1].tolist()
    (na_q, na_k, na_v, na_z, ml_q, ml_k, ml_v, ml_o, ml_z,
     i_f, f_f, i_b, f_b) = jnp.split(proj, split_idx, axis=-1)

    na_out = neighbourhood_attention(na_q, na_k, na_v, rpb) * jax.nn.silu(na_z)

    qk = jax.nn.silu(centred_short_conv(jnp.concatenate([ml_q, ml_k], axis=-1), conv_w, conv_b))
    mq, mk = jnp.split(qk, 2, axis=-1)

    def heads(t):
        return t.reshape(B, T, ML_HEADS, ML_HEAD_DIM).transpose(0, 2, 1, 3)

    mq, mk, mv = heads(mq), heads(mk), heads(ml_v)
    i_f, f_f, i_b, f_b = (g.transpose(0, 2, 1) for g in (i_f, f_f, i_b, f_b))
    h_fwd = mlstm_chunkwise(mq, mk, mv, i_f, f_f)
    flip = lambda t: jnp.flip(t, axis=2)
    h_bwd = flip(mlstm_chunkwise(flip(mq), flip(mk), flip(mv), flip(i_b), flip(f_b)))
    hm = (h_fwd + h_bwd).transpose(0, 2, 1, 3) * jax.nn.sigmoid(ml_o).reshape(B, T, ML_HEADS, ML_HEAD_DIM)
    mu = jnp.mean(hm, axis=-1, keepdims=True)
    var = jnp.mean(jnp.square(hm - mu), axis=-1, keepdims=True)
    hm = ((hm - mu) * lax.rsqrt(var + EPS)).reshape(B, T, ML_W) * ml_norm_w
    ml_out = hm * jax.nn.silu(ml_z)

    return jnp.concatenate([na_out, ml_out], axis=-1) @ w_out


def setup_inputs(seed: int = 0) -> dict:
    key = jax.random.key(seed)
    ks = jax.random.split(key, 14)
    D = D_MODEL
    x = jax.random.normal(ks[0], (BATCH, SEQ, D), jnp.float32)
    c = jax.random.normal(ks[1], (BATCH, D), jnp.float32)
    w_ada = jax.random.normal(ks[2], (DEPTH, D, 3 * D), jnp.float32) * (0.5 * D ** -0.5)
    b_ada = jax.random.normal(ks[3], (DEPTH, 3 * D), jnp.float32) * 0.02
    norm_w = 1.0 + 0.1 * jax.random.normal(ks[4], (DEPTH, D), jnp.float32)
    w_in = jax.random.normal(ks[5], (DEPTH, D, IN_W), jnp.float32) * (D ** -0.5)
    n_main = 4 * NA_W + 5 * ML_W
    fgate_base = jnp.linspace(3.0, 6.0, ML_HEADS, dtype=jnp.float32)
    gk = jax.random.split(ks[6], 5)
    b_in = jnp.concatenate([
        0.02 * jax.random.normal(gk[0], (DEPTH, n_main), jnp.float32),
        0.1 * jax.random.normal(gk[1], (DEPTH, ML_HEADS), jnp.float32),
        fgate_base + 0.1 * jax.random.normal(gk[2], (DEPTH, ML_HEADS), jnp.float32),
        0.1 * jax.random.normal(gk[3], (DEPTH, ML_HEADS), jnp.float32),
        fgate_base + 0.1 * jax.random.normal(gk[4], (DEPTH, ML_HEADS), jnp.float32),
    ], axis=-1)
    conv_w = jax.random.normal(ks[7], (DEPTH, CONV_W, 2 * ML_W), jnp.float32) * (CONV_W ** -0.5)
    conv_b = 0.02 * jax.random.normal(ks[8], (DEPTH, 2 * ML_W), jnp.float32)
    rpb = 0.1 * jax.random.normal(ks[9], (DEPTH, NA_HEADS, 2 * NA_KH - 1, 2 * NA_KW - 1), jnp.float32)
    ml_norm_w = 1.0 + 0.1 * jax.random.normal(ks[10], (DEPTH, ML_W), jnp.float32)
    w_out = jax.random.normal(ks[11], (DEPTH, MIX_W, D), jnp.float32) * (MIX_W ** -0.5)
    final_norm_w = 1.0 + 0.1 * jax.random.normal(ks[12], (D,), jnp.float32)
    return {'x': x, 'c': c, 'w_ada': w_ada, 'b_ada': b_ada, 'norm_w': norm_w, 'w_in': w_in,
            'b_in': b_in, 'conv_w': conv_w, 'conv_b': conv_b, 'rpb': rpb, 'ml_norm_w': ml_norm_w,
            'w_out': w_out, 'final_norm_w': final_norm_w}


def reference(x, c, w_ada, b_ada, norm_w, w_in, b_in, conv_w, conv_b, rpb, ml_norm_w, w_out, final_norm_w):
    h_res = x.astype(jnp.float32)
    c_act = jax.nn.silu(c.astype(jnp.float32))
    for l in range(DEPTH):
        mod = c_act @ w_ada[l] + b_ada[l]
        shift, scale, gate = jnp.split(mod, 3, axis=-1)
        h = rmsnorm(h_res, norm_w[l]) * (1.0 + scale[:, None, :]) + shift[:, None, :]
        y = hybrid_mixer(h, w_in[l], b_in[l], conv_w[l], conv_b[l], rpb[l], ml_norm_w[l], w_out[l])
        h_res = h_res + gate[:, None, :] * y
    return rmsnorm(h_res, final_norm_w).astype(x.dtype)
```

```python
import functools

import jax
import jax.numpy as jnp
from jax import lax
from jax.experimental import pallas as pl
from jax.experimental.pallas import tpu as pltpu

GRID_W = 64
NA_HEADS = 8
NA_HEAD_DIM = 64
NA_KH = 8
NA_KW = 16
NA_W = NA_HEADS * NA_HEAD_DIM
ML_HEADS = 4
ML_HEAD_DIM = 128
ML_W = ML_HEADS * ML_HEAD_DIM
CONV_W = 5
EPS = 1e-6

LANES = 128
HALO = 8
ML_CHUNK = 256
NA_ROWS_PER_STEP = 8
VMEM_LIMIT = 56 * 1024 * 1024

COL_NA_Q, COL_NA_K, COL_NA_V, COL_NA_Z, COL_ML_Q, COL_ML_K, COL_ML_V, COL_ML_O, COL_ML_Z = range(9)
N_COLS = 9
BW = 512

G_B, G_G, G_CM = 0, 2 * ML_HEADS, 4 * ML_HEADS
G_DIR = ML_HEADS
G_ROWS = 32


def _silu(v):
    return v * jax.nn.sigmoid(v)


def _log_sigmoid(v):
    return -(jnp.maximum(-v, 0.0) + jnp.log1p(jnp.exp(-jnp.abs(v))))


def _ada_kernel(c_ref, w_ref, b_ref, o_ref):
    s = _silu(c_ref[...])
    o_ref[...] = jnp.sum(w_ref[...] * s, axis=0, keepdims=True) + b_ref[...]


def _ada_call(c_col, w_ada, b_ada):
    d, n = w_ada.shape
    tn = 512
    return pl.pallas_call(
        _ada_kernel,
        grid=(n // tn,),
        in_specs=[pl.BlockSpec((d, 1), lambda j: (0, 0)),
                  pl.BlockSpec((d, tn), lambda j: (0, j)),
                  pl.BlockSpec((1, tn), lambda j: (0, j))],
        out_specs=pl.BlockSpec((1, tn), lambda j: (0, j)),
        out_shape=jax.ShapeDtypeStruct((1, n), jnp.float32),
        name="ada_mod",
    )(c_col, w_ada, b_ada)


def _seg_scan(x, op, ident, reverse):
    n = x.shape[1]
    pos = lax.broadcasted_iota(jnp.int32, x.shape, 1) % ML_CHUNK
    k = 1
    while k < ML_CHUNK:
        if reverse:
            y = pltpu.roll(x, n - k, axis=1)
            ok = pos < ML_CHUNK - k
        else:
            y = pltpu.roll(x, k, axis=1)
            ok = pos >= k
        x = op(x, jnp.where(ok, y, ident))
        k *= 2
    return x


def _inproj_kernel(x_ref, xp_ref, xn_ref, mod_ref, nw_ref, wmain_ref, wqk_ref, bmain_ref, bqk_ref,
                   wg_ref, bg_ref, cw_ref, cb_ref, a_ref, grow_ref, gcol_ref):
    i = pl.program_id(0)
    nsteps = pl.num_programs(0)
    tm, d = x_ref.shape
    shift = mod_ref[:, 0:d]
    amp = nw_ref[...] * (1.0 + mod_ref[:, d:2 * d])

    def norm(xv):
        ms = jnp.mean(xv * xv, axis=-1, keepdims=True)
        return xv * lax.rsqrt(ms + EPS) * amp + shift

    h = norm(x_ref[...])
    hb = h.astype(jnp.bfloat16)

    main = jnp.dot(hb, wmain_ref[...], preferred_element_type=jnp.float32) + bmain_ref[...]
    q_scale = NA_HEAD_DIM ** -0.5
    a_ref[:, COL_NA_Q * BW:(COL_NA_Q + 1) * BW] = (main[:, 0:BW] * q_scale).astype(a_ref.dtype)
    a_ref[:, COL_NA_K * BW:(COL_NA_Z + 1) * BW] = main[:, BW:4 * BW].astype(a_ref.dtype)
    a_ref[:, COL_ML_V * BW:(COL_ML_Z + 1) * BW] = main[:, 4 * BW:7 * BW].astype(a_ref.dtype)

    hext = jnp.concatenate([norm(xp_ref[...]), h, norm(xn_ref[...])], axis=0).astype(jnp.bfloat16)
    pqk = jnp.dot(hext, wqk_ref[...], preferred_element_type=jnp.float32) + bqk_ref[...]
    row = lax.broadcasted_iota(jnp.int32, (tm + 2 * HALO, 1), 0)
    inside = ((row >= HALO) | (i > 0)) & ((row < tm + HALO) | (i < nsteps - 1))
    pqk = jnp.where(inside, pqk, 0.0)
    pad = CONV_W // 2
    acc = pqk[HALO - pad:HALO - pad + tm] * cw_ref[0:1, :]
    for j in range(1, CONV_W):
        acc = acc + pqk[HALO - pad + j:HALO - pad + j + tm] * cw_ref[j:j + 1, :]
    qk = _silu(acc + cb_ref[...])
    a_ref[:, COL_ML_Q * BW:(COL_ML_Q + 1) * BW] = qk[:, 0:BW].astype(a_ref.dtype)
    a_ref[:, COL_ML_K * BW:(COL_ML_K + 1) * BW] = (qk[:, BW:2 * BW] * (ML_HEAD_DIM ** -0.5)).astype(a_ref.dtype)

    g = lax.dot_general(wg_ref[...], hb, (((1,), (1,)), ((), ())),
                        preferred_element_type=jnp.float32) + bg_ref[...]
    nh2 = 2 * ML_HEADS
    is_fwd = lax.broadcasted_iota(jnp.int32, (nh2, tm), 0) < ML_HEADS

    def scan_both(v, op, ident):
        return jnp.where(is_fwd, _seg_scan(v, op, ident, False), _seg_scan(v, op, ident, True))

    b = scan_both(_log_sigmoid(g[nh2:2 * nh2]), jnp.add, 0.0)
    gg = g[0:nh2] - b
    cm = scan_both(gg, jnp.maximum, -jnp.inf)
    gr = jnp.concatenate([b, gg, cm, jnp.zeros((G_ROWS - 3 * nh2, tm), jnp.float32)], axis=0)
    grow_ref[...] = gr
    gcol_ref[...] = gr.T


def _inproj_call(x, mod, norm_w, w_main, w_qk, b_main, b_qk, w_g, b_g, conv_w, conv_b, tm):
    t, d = x.shape
    n_main = w_main.shape[1]
    hb = tm // HALO
    nhb = t // HALO
    const = lambda i: (0, 0)
    return pl.pallas_call(
        _inproj_kernel,
        grid=(t // tm,),
        in_specs=[pl.BlockSpec((tm, d), lambda i: (i, 0)),
                  pl.BlockSpec((HALO, d), lambda i: (jnp.maximum(i * hb - 1, 0), 0)),
                  pl.BlockSpec((HALO, d), lambda i: (jnp.minimum((i + 1) * hb, nhb - 1), 0)),
                  pl.BlockSpec(mod.shape, const),
                  pl.BlockSpec((1, d), const),
                  pl.BlockSpec((d, n_main), const),
                  pl.BlockSpec((d, 2 * BW), const),
                  pl.BlockSpec((1, n_main), const),
                  pl.BlockSpec((1, 2 * BW), const),
                  pl.BlockSpec(w_g.shape, const),
                  pl.BlockSpec(b_g.shape, const),
                  pl.BlockSpec(conv_w.shape, const),
                  pl.BlockSpec(conv_b.shape, const)],
        out_specs=[pl.BlockSpec((tm, N_COLS * BW), lambda i: (i, 0)),
                   pl.BlockSpec((G_ROWS, tm), lambda i: (0, i)),
                   pl.BlockSpec((tm, G_ROWS), lambda i: (i, 0))],
        out_shape=[jax.ShapeDtypeStruct((t, N_COLS * BW), jnp.bfloat16),
                   jax.ShapeDtypeStruct((G_ROWS, t), jnp.float32),
                   jax.ShapeDtypeStruct((t, G_ROWS), jnp.float32)],
        compiler_params=pltpu.CompilerParams(dimension_semantics=("arbitrary",),
                                             vmem_limit_bytes=VMEM_LIMIT),
        name="in_proj",
    )(x, x, x, mod, norm_w, w_main, w_qk, b_main, b_qk, w_g, b_g, conv_w, conv_b)


def _na_kernel(rpb_ref, q_ref, kp_ref, kc_ref, kn_ref, vp_ref, vc_ref, vn_ref, z_ref, o_ref,
               bias_ref, kwin_ref, vwin_ref, *, grid_rows):
    i = pl.program_id(0)
    blk = NA_ROWS_PER_STEP * GRID_W
    n_dy = 2 * NA_KH - 1

    @pl.when(i == 0)
    def _():
        cq = lax.broadcasted_iota(jnp.int32, (GRID_W, LANES), 0)
        lane = lax.broadcasted_iota(jnp.int32, (GRID_W, LANES), 1)
        ck = lane % GRID_W
        cs = jnp.clip(cq - NA_KW // 2, 0, GRID_W - NA_KW)
        valid = (ck >= cs) & (ck < cs + NA_KW)
        lo = lane < GRID_W
        base_shift = LANES - (NA_KW - 1)

        def build(idx, carry):
            delta = idx // NA_HEADS
            head = idx % NA_HEADS
            for jj in range(NA_KH // 2):
                r0 = head * n_dy + delta + 2 * jj
                p0 = jnp.broadcast_to(rpb_ref[pl.ds(r0, 1), :], (GRID_W, LANES))
                p1 = jnp.broadcast_to(rpb_ref[pl.ds(r0 + 1, 1), :], (GRID_W, LANES))
                t0 = pltpu.roll(p0, base_shift, axis=1, stride=1, stride_axis=0)
                t1 = pltpu.roll(p1, (base_shift + GRID_W) % LANES, axis=1, stride=1, stride_axis=0)
                tile = jnp.where(valid, jnp.where(lo, t0, t1), -jnp.inf)
                bias_ref[delta, head, :, jj * LANES:(jj + 1) * LANES] = tile
            return carry

        lax.fori_loop(0, NA_KH * NA_HEADS, build, 0)

    kwin_ref[0:blk, :] = kp_ref[...]
    kwin_ref[blk:2 * blk, :] = kc_ref[...]
    kwin_ref[2 * blk:3 * blk, :] = kn_ref[...]
    vwin_ref[0:blk, :] = vp_ref[...]
    vwin_ref[blk:2 * blk, :] = vc_ref[...]
    vwin_ref[2 * blk:3 * blk, :] = vn_ref[...]

    lane_lo = lax.broadcasted_iota(jnp.int32, (1, LANES), 1) < NA_HEAD_DIM

    def row_body(rl, carry):
        r = i * NA_ROWS_PER_STEP + rl
        start = jnp.clip(r - NA_KH // 2, 0, grid_rows - NA_KH)
        delta = start - r + NA_KH - 1
        woff = pl.multiple_of((start - (i - 1) * NA_ROWS_PER_STEP) * GRID_W, GRID_W)
        qoff = pl.multiple_of(rl * GRID_W, GRID_W)
        for p in range(NA_HEADS // 2):
            cols = slice(p * LANES, (p + 1) * LANES)
            qp = q_ref[pl.ds(qoff, GRID_W), cols]
            kw = kwin_ref[pl.ds(woff, NA_KH * GRID_W), cols]
            vw = vwin_ref[pl.ds(woff, NA_KH * GRID_W), cols]
            o_pair = None
            for e in range(2):
                sel = lane_lo if e == 0 else jnp.logical_not(lane_lo)
                qm = jnp.where(sel, qp, jnp.zeros_like(qp))
                s = lax.dot_general(qm, kw, (((1,), (1,)), ((), ())), preferred_element_type=jnp.float32)
                s = s + bias_ref[delta, 2 * p + e]
                m = jnp.max(s, axis=-1, keepdims=True)
                pe = jnp.exp(s - m)
                l = jnp.sum(pe, axis=-1, keepdims=True)
                vm = jnp.where(sel, vw, jnp.zeros_like(vw))
                oe = jnp.dot(pe.astype(vm.dtype), vm, preferred_element_type=jnp.float32) * (1.0 / l)
                o_pair = oe if o_pair is None else o_pair + oe
            z = z_ref[pl.ds(qoff, GRID_W), cols].astype(jnp.float32)
            o_ref[pl.ds(qoff, GRID_W), cols] = (o_pair * _silu(z)).astype(o_ref.dtype)
        return carry

    lax.fori_loop(0, NA_ROWS_PER_STEP, row_body, 0)


def _na_call(rpb_rows, a):
    t = a.shape[0]
    grid_rows = t // GRID_W
    blk = NA_ROWS_PER_STEP * GRID_W
    nblk = t // blk
    cur = lambda c: (lambda i: (i, c))
    prev = lambda c: (lambda i: (jnp.maximum(i - 1, 0), c))
    nxt = lambda c: (lambda i: (jnp.minimum(i + 1, nblk - 1), c))
    tok = lambda im: pl.BlockSpec((blk, BW), im)
    return pl.pallas_call(
        functools.partial(_na_kernel, grid_rows=grid_rows),
        grid=(nblk,),
        in_specs=[pl.BlockSpec(rpb_rows.shape, lambda i: (0, 0)),
                  tok(cur(COL_NA_Q)),
                  tok(prev(COL_NA_K)), tok(cur(COL_NA_K)), tok(nxt(COL_NA_K)),
                  tok(prev(COL_NA_V)), tok(cur(COL_NA_V)), tok(nxt(COL_NA_V)),
                  tok(cur(COL_NA_Z))],
        out_specs=pl.BlockSpec((blk, BW), lambda i: (i, 0)),
        out_shape=jax.ShapeDtypeStruct((t, BW), jnp.bfloat16),
        scratch_shapes=[pltpu.VMEM((NA_KH, NA_HEADS, GRID_W, NA_KH * GRID_W), jnp.float32),
                        pltpu.VMEM((3 * blk, BW), jnp.bfloat16),
                        pltpu.VMEM((3 * blk, BW), jnp.bfloat16)],
        compiler_params=pltpu.CompilerParams(dimension_semantics=("arbitrary",),
                                             vmem_limit_bytes=VMEM_LIMIT),
        name="na_attn",
    )(rpb_rows, a, a, a, a, a, a, a, a)


def _mlstm_kernel(qf_ref, kf_ref, vf_ref, grf_ref, gcf_ref, qb_ref, kb_ref, vb_ref, grb_ref, gcb_ref,
                  hf_ref, hb_ref, ct_ref, mu_ref):
    j = pl.program_id(0)
    lc = ML_CHUNK
    hd = ML_HEAD_DIM

    @pl.when(j == 0)
    def _():
        ct_ref[...] = jnp.zeros_like(ct_ref)
        mu_ref[...] = jnp.zeros_like(mu_ref)

    t_idx = lax.broadcasted_iota(jnp.int32, (lc, lc), 0)
    s_idx = lax.broadcasted_iota(jnp.int32, (lc, lc), 1)
    ones_col = (lax.broadcasted_iota(jnp.int32, (lc, hd), 1) == 0).astype(jnp.bfloat16)

    dirs = ((qf_ref, kf_ref, vf_ref, grf_ref, gcf_ref, hf_ref, s_idx <= t_idx, lc - 1, 0),
            (qb_ref, kb_ref, vb_ref, grb_ref, gcb_ref, hb_ref, s_idx >= t_idx, 0, G_DIR))
    for dirn, (q_ref, k_ref, v_ref, gr_ref, gc_ref, h_ref, mask, last, base) in enumerate(dirs):
        for hh in range(ML_HEADS):
            u = dirn * ML_HEADS + hh
            cols = slice(hh * hd, (hh + 1) * hd)
            q = q_ref[:, cols]
            k = k_ref[:, cols]
            v = v_ref[:, cols]
            g_row = gr_ref[base + G_G + hh:base + G_G + hh + 1, :]
            b_col = gc_ref[:, base + G_B + hh:base + G_B + hh + 1]
            g_col = gc_ref[:, base + G_G + hh:base + G_G + hh + 1]
            cm_col = gc_ref[:, base + G_CM + hh:base + G_CM + hh + 1]
            b_end = gc_ref[last:last + 1, base + G_B + hh:base + G_B + hh + 1]
            cm_end = gc_ref[last:last + 1, base + G_CM + hh:base + G_CM + hh + 1]
            mu = mu_ref[u][0:1, 0:1]

            m_run = jnp.maximum(mu, cm_col)
            m_end = jnp.maximum(mu, cm_end)
            pw = jnp.exp(jnp.where(mask, g_row - m_run, -jnp.inf))
            s = lax.dot_general(q, k, (((1,), (1,)), ((), ())), preferred_element_type=jnp.float32)
            a = (s * pw).astype(jnp.bfloat16)
            vext = jnp.concatenate([v, ones_col], axis=1)
            r1 = jnp.dot(a, vext, preferred_element_type=jnp.float32)
            ct = ct_ref[u]
            r2 = jnp.dot(q, ct.astype(jnp.bfloat16), preferred_element_type=jnp.float32)
            sc = jnp.exp(mu - m_run)
            num = r1[:, 0:hd] + sc * r2[:, 0:hd]
            den = r1[:, hd:hd + 1] + sc * r2[:, hd:hd + 1]
            bound = jnp.exp(-(b_col + m_run))
            h_ref[:, cols] = (num * (1.0 / jnp.maximum(jnp.abs(den), bound))).astype(h_ref.dtype)

            w_col = jnp.exp(g_col - m_end)
            wv = (w_col * vext.astype(jnp.float32)).astype(jnp.bfloat16)
            upd = lax.dot_general(k, wv, (((0,), (0,)), ((), ())), preferred_element_type=jnp.float32)
            ct_ref[u] = jnp.exp(mu - m_end) * ct + upd
            mu_ref[u] = jnp.broadcast_to(b_end + m_end, mu_ref.shape[1:])


def _mlstm_call(a, grow, gcol):
    t = a.shape[0]
    lc = ML_CHUNK
    nc = t // lc
    fwd = lambda c: (lambda j: (j, c))
    bwd = lambda c: (lambda j: (nc - 1 - j, c))
    tok = lambda im: pl.BlockSpec((lc, BW), im)
    return pl.pallas_call(
        _mlstm_kernel,
        grid=(nc,),
        in_specs=[tok(fwd(COL_ML_Q)), tok(fwd(COL_ML_K)), tok(fwd(COL_ML_V)),
                  pl.BlockSpec((G_ROWS, lc), lambda j: (0, j)),
                  pl.BlockSpec((lc, G_ROWS), lambda j: (j, 0)),
                  tok(bwd(COL_ML_Q)), tok(bwd(COL_ML_K)), tok(bwd(COL_ML_V)),
                  pl.BlockSpec((G_ROWS, lc), lambda j: (0, nc - 1 - j)),
                  pl.BlockSpec((lc, G_ROWS), lambda j: (nc - 1 - j, 0))],
        out_specs=[pl.BlockSpec((lc, BW), lambda j: (j, 0)),
                   pl.BlockSpec((lc, BW), lambda j: (nc - 1 - j, 0))],
        out_shape=[jax.ShapeDtypeStruct((t, BW), jnp.float32),
                   jax.ShapeDtypeStruct((t, BW), jnp.float32)],
        scratch_shapes=[pltpu.VMEM((2 * ML_HEADS, ML_HEAD_DIM, 2 * ML_HEAD_DIM), jnp.float32),
                        pltpu.VMEM((2 * ML_HEADS, 8, LANES), jnp.float32)],
        compiler_params=pltpu.CompilerParams(dimension_semantics=("arbitrary",),
                                             vmem_limit_bytes=VMEM_LIMIT),
        name="mlstm",
    )(a, a, a, grow, gcol, a, a, a, grow, gcol)


def _outproj_kernel(na_ref, hf_ref, hb_ref, o_ref, z_ref, mlw_ref, wout_ref, x_ref, mod_ref, fnw_ref, out_ref):
    d = x_ref.shape[1]
    hd = ML_HEAD_DIM
    hm = (hf_ref[...] + hb_ref[...]) * jax.nn.sigmoid(o_ref[...].astype(jnp.float32))
    parts = []
    for hh in range(ML_HEADS):
        seg = hm[:, hh * hd:(hh + 1) * hd]
        dev = seg - jnp.mean(seg, axis=-1, keepdims=True)
        var = jnp.mean(dev * dev, axis=-1, keepdims=True)
        parts.append(dev * lax.rsqrt(var + EPS))
    ml = jnp.concatenate(parts, axis=1) * mlw_ref[...] * _silu(z_ref[...].astype(jnp.float32))
    y = jnp.dot(na_ref[...], wout_ref[0:NA_W, :], preferred_element_type=jnp.float32)
    y = y + jnp.dot(ml.astype(jnp.bfloat16), wout_ref[NA_W:NA_W + ML_W, :], preferred_element_type=jnp.float32)
    res = x_ref[...] + mod_ref[:, 2 * d:3 * d] * y
    ms = jnp.mean(res * res, axis=-1, keepdims=True)
    out_ref[...] = (res * lax.rsqrt(ms + EPS) * fnw_ref[...]).astype(out_ref.dtype)


def _outproj_call(na_out, hf, hb, a, ml_norm_w, w_out, x, mod, fnw, tm):
    t, d = x.shape
    const = lambda i: (0, 0)
    return pl.pallas_call(
        _outproj_kernel,
        grid=(t // tm,),
        in_specs=[pl.BlockSpec((tm, BW), lambda i: (i, 0)),
                  pl.BlockSpec((tm, BW), lambda i: (i, 0)),
                  pl.BlockSpec((tm, BW), lambda i: (i, 0)),
                  pl.BlockSpec((tm, BW), lambda i: (i, COL_ML_O)),
                  pl.BlockSpec((tm, BW), lambda i: (i, COL_ML_Z)),
                  pl.BlockSpec((1, ML_W), const),
                  pl.BlockSpec(w_out.shape, const),
                  pl.BlockSpec((tm, d), lambda i: (i, 0)),
                  pl.BlockSpec(mod.shape, const),
                  pl.BlockSpec((1, d), const)],
        out_specs=pl.BlockSpec((tm, d), lambda i: (i, 0)),
        out_shape=jax.ShapeDtypeStruct((t, d), x.dtype),
        compiler_params=pltpu.CompilerParams(dimension_semantics=("arbitrary",),
                                             vmem_limit_bytes=VMEM_LIMIT),
        name="out_proj",
    )(na_out, hf, hb, a, a, ml_norm_w, w_out, x, mod, fnw)


def _layer(x, c_col, w_ada, b_ada, norm_w, w_in, b_in, conv_w, conv_b, rpb, ml_norm_w, w_out, fnw, tm):
    d = x.shape[1]
    bf = jnp.bfloat16
    mod = _ada_call(c_col, w_ada, b_ada[None, :])

    q0, q1, g0 = 4 * NA_W, 4 * NA_W + 2 * ML_W, 4 * NA_W + 5 * ML_W
    w_main = jnp.concatenate([w_in[:, :q0], w_in[:, q1:g0]], axis=1).astype(bf)
    b_main = jnp.concatenate([b_in[:q0], b_in[q1:g0]])[None, :]
    w_qk = w_in[:, q0:q1].astype(bf)
    b_qk = b_in[q0:q1][None, :]
    nh = ML_HEADS
    perm = jnp.array(list(range(0, nh)) + list(range(2 * nh, 3 * nh)) + list(range(nh, 2 * nh))
                     + list(range(3 * nh, 4 * nh)), jnp.int32)
    w_g = w_in[:, g0:].T[perm].astype(bf)
    b_g = b_in[g0:][perm][:, None]

    a, grow, gcol = _inproj_call(x, mod, norm_w[None, :], w_main, w_qk, b_main, b_qk, w_g, b_g,
                                 conv_w, conv_b[None, :], tm)

    n_dy, n_dx = 2 * NA_KH - 1, 2 * NA_KW - 1
    rpb_rows = jnp.pad(rpb.reshape(NA_HEADS * n_dy, n_dx), ((0, 0), (0, LANES - n_dx)))
    na_out = _na_call(rpb_rows, a)
    hf, hb = _mlstm_call(a, grow, gcol)
    return _outproj_call(na_out, hf, hb, a, ml_norm_w[None, :], w_out.astype(bf), x, mod, fnw[None, :], tm)


def kernel(x, c, w_ada, b_ada, norm_w, w_in, b_in, conv_w, conv_b, rpb, ml_norm_w, w_out, final_norm_w):
    batch, t, d = x.shape
    depth = w_ada.shape[0]
    assert batch == 1 and depth == 1, "one sequence, one layer"
    assert t % (NA_ROWS_PER_STEP * GRID_W) == 0 and t % ML_CHUNK == 0
    tm = 512 if t % 512 == 0 else ML_CHUNK
    out = _layer(x[0].astype(jnp.float32), c.astype(jnp.float32).reshape(d, 1), w_ada[0], b_ada[0], norm_w[0],
                 w_in[0], b_in[0], conv_w[0], conv_b[0], rpb[0], ml_norm_w[0], w_out[0], final_norm_w, tm)
    return out[None].astype(x.dtype)
```

```python
import jax
import jax.numpy as jnp
from jax import lax
from jax.experimental import pallas as pl
from jax.experimental.pallas import tpu as pltpu

GRID_W = 64
NA_HEADS = 8
NA_HEAD_DIM = 64
NA_KH = 8
NA_KW = 16
NA_W = NA_HEADS * NA_HEAD_DIM
ML_HEADS = 4
ML_HEAD_DIM = 128
ML_W = ML_HEADS * ML_HEAD_DIM
CONV_W = 5
EPS = 1e-6

LANES = 128
HALO = 8
ML_CHUNK = 256
NA_ROWS_PER_STEP = 4
NA_WIN_ROWS = 3 * NA_ROWS_PER_STEP
VMEM_LIMIT = 56 * 1024 * 1024

COL_NA_Q, COL_NA_K, COL_NA_Z, COL_ML_Q, COL_ML_K, COL_ML_V, COL_ML_O, COL_ML_Z = range(8)
N_COLS = 8
BW = 512

G_B, G_G, G_CM = 0, 2 * ML_HEADS, 4 * ML_HEADS
G_DIR = ML_HEADS
G_ROWS = 32


def _silu(v):
    return v * jax.nn.sigmoid(v)


def _log_sigmoid(v):
    return -(jnp.maximum(-v, 0.0) + jnp.log1p(jnp.exp(-jnp.abs(v))))


def _ada_kernel(c_ref, w_ref, b_ref, o_ref):
    s = _silu(c_ref[...])
    o_ref[...] = jnp.sum(w_ref[...] * s, axis=0, keepdims=True) + b_ref[...]


def _ada_call(c_col, w_ada, b_ada):
    d, n = w_ada.shape
    tn = 512
    return pl.pallas_call(
        _ada_kernel,
        grid=(n // tn,),
        in_specs=[pl.BlockSpec((d, 1), lambda j: (0, 0)),
                  pl.BlockSpec((d, tn), lambda j: (0, j)),
                  pl.BlockSpec((1, tn), lambda j: (0, j))],
        out_specs=pl.BlockSpec((1, tn), lambda j: (0, j)),
        out_shape=jax.ShapeDtypeStruct((1, n), jnp.float32),
        name="ada_mod",
    )(c_col, w_ada, b_ada)


def _seg_scan(x, op, ident, reverse):
    n = x.shape[1]
    pos = lax.broadcasted_iota(jnp.int32, x.shape, 1) % ML_CHUNK
    k = 1
    while k < ML_CHUNK:
        if reverse:
            y = pltpu.roll(x, n - k, axis=1)
            ok = pos < ML_CHUNK - k
        else:
            y = pltpu.roll(x, k, axis=1)
            ok = pos >= k
        x = op(x, jnp.where(ok, y, ident))
        k *= 2
    return x


def _inproj_kernel(x_ref, xp_ref, xn_ref, mod_ref, nw_ref, wmain_ref, wqk_ref, wvt_ref, bmain_ref, bqk_ref,
                   bvt_ref, wg_ref, bg_ref, cw_ref, cb_ref, a_ref, vt_ref, grow_ref, gcol_ref):
    i = pl.program_id(0)
    nsteps = pl.num_programs(0)
    tm, d = x_ref.shape
    shift = mod_ref[:, 0:d]
    amp = nw_ref[...] * (1.0 + mod_ref[:, d:2 * d])

    def norm(xv):
        ms = jnp.mean(xv * xv, axis=-1, keepdims=True)
        return xv * lax.rsqrt(ms + EPS) * amp + shift

    h = norm(x_ref[...])
    hb = h.astype(jnp.bfloat16)

    main = jnp.dot(hb, wmain_ref[...], preferred_element_type=jnp.float32) + bmain_ref[...]
    q_scale = NA_HEAD_DIM ** -0.5
    a_ref[:, COL_NA_Q * BW:(COL_NA_Q + 1) * BW] = (main[:, 0:BW] * q_scale).astype(a_ref.dtype)
    a_ref[:, COL_NA_K * BW:(COL_NA_Z + 1) * BW] = main[:, BW:3 * BW].astype(a_ref.dtype)
    a_ref[:, COL_ML_V * BW:(COL_ML_Z + 1) * BW] = main[:, 3 * BW:6 * BW].astype(a_ref.dtype)

    vt = lax.dot_general(wvt_ref[...], hb, (((1,), (1,)), ((), ())), preferred_element_type=jnp.float32)
    vt_ref[...] = (vt + bvt_ref[...]).astype(vt_ref.dtype)

    hext = jnp.concatenate([norm(xp_ref[...]), h, norm(xn_ref[...])], axis=0).astype(jnp.bfloat16)
    pqk = jnp.dot(hext, wqk_ref[...], preferred_element_type=jnp.float32) + bqk_ref[...]
    row = lax.broadcasted_iota(jnp.int32, (tm + 2 * HALO, 1), 0)
    inside = ((row >= HALO) | (i > 0)) & ((row < tm + HALO) | (i < nsteps - 1))
    pqk = jnp.where(inside, pqk, 0.0)
    pad = CONV_W // 2
    acc = pqk[HALO - pad:HALO - pad + tm] * cw_ref[0:1, :]
    for j in range(1, CONV_W):
        acc = acc + pqk[HALO - pad + j:HALO - pad + j + tm] * cw_ref[j:j + 1, :]
    qk = _silu(acc + cb_ref[...])
    a_ref[:, COL_ML_Q * BW:(COL_ML_Q + 1) * BW] = qk[:, 0:BW].astype(a_ref.dtype)
    a_ref[:, COL_ML_K * BW:(COL_ML_K + 1) * BW] = (qk[:, BW:2 * BW] * (ML_HEAD_DIM ** -0.5)).astype(a_ref.dtype)

    g = lax.dot_general(wg_ref[...], hb, (((1,), (1,)), ((), ())),
                        preferred_element_type=jnp.float32) + bg_ref[...]
    nh2 = 2 * ML_HEADS
    is_fwd = lax.broadcasted_iota(jnp.int32, (nh2, tm), 0) < ML_HEADS

    def scan_both(v, op, ident):
        return jnp.where(is_fwd, _seg_scan(v, op, ident, False), _seg_scan(v, op, ident, True))

    b = scan_both(_log_sigmoid(g[nh2:2 * nh2]), jnp.add, 0.0)
    gg = g[0:nh2] - b
    cm = scan_both(gg, jnp.maximum, -jnp.inf)
    gr = jnp.concatenate([b, gg, cm, jnp.zeros((G_ROWS - 3 * nh2, tm), jnp.float32)], axis=0)
    grow_ref[...] = gr
    gcol_ref[...] = gr.T


def _inproj_call(x, mod, norm_w, w_main, w_qk, w_vt, b_main, b_qk, b_vt, w_g, b_g, conv_w, conv_b, tm):
    t, d = x.shape
    n_main = w_main.shape[1]
    hb = tm // HALO
    nhb = t // HALO
    const = lambda i: (0, 0)
    return pl.pallas_call(
        _inproj_kernel,
        grid=(t // tm,),
        in_specs=[pl.BlockSpec((tm, d), lambda i: (i, 0)),
                  pl.BlockSpec((HALO, d), lambda i: (jnp.maximum(i * hb - 1, 0), 0)),
                  pl.BlockSpec((HALO, d), lambda i: (jnp.minimum((i + 1) * hb, nhb - 1), 0)),
                  pl.BlockSpec(mod.shape, const),
                  pl.BlockSpec((1, d), const),
                  pl.BlockSpec((d, n_main), const),
                  pl.BlockSpec((d, 2 * BW), const),
                  pl.BlockSpec(w_vt.shape, const),
                  pl.BlockSpec((1, n_main), const),
                  pl.BlockSpec((1, 2 * BW), const),
                  pl.BlockSpec(b_vt.shape, const),
                  pl.BlockSpec(w_g.shape, const),
                  pl.BlockSpec(b_g.shape, const),
                  pl.BlockSpec(conv_w.shape, const),
                  pl.BlockSpec(conv_b.shape, const)],
        out_specs=[pl.BlockSpec((tm, N_COLS * BW), lambda i: (i, 0)),
                   pl.BlockSpec((NA_W, tm), lambda i: (0, i)),
                   pl.BlockSpec((G_ROWS, tm), lambda i: (0, i)),
                   pl.BlockSpec((tm, G_ROWS), lambda i: (i, 0))],
        out_shape=[jax.ShapeDtypeStruct((t, N_COLS * BW), jnp.bfloat16),
                   jax.ShapeDtypeStruct((NA_W, t), jnp.bfloat16),
                   jax.ShapeDtypeStruct((G_ROWS, t), jnp.float32),
                   jax.ShapeDtypeStruct((t, G_ROWS), jnp.float32)],
        compiler_params=pltpu.CompilerParams(dimension_semantics=("arbitrary",),
                                             vmem_limit_bytes=VMEM_LIMIT),
        name="in_proj",
    )(x, x, x, mod, norm_w, w_main, w_qk, w_vt, b_main, b_qk, b_vt, w_g, b_g, conv_w, conv_b)


def _na_kernel(rpb_ref, q_ref, kp_ref, kc_ref, kn_ref, vp_ref, vc_ref, vn_ref, z_ref, o_ref, tbl_ref):
    i = pl.program_id(0)
    nsteps = pl.num_programs(0)
    blk = NA_ROWS_PER_STEP * GRID_W
    n_dy = 2 * NA_KH - 1
    first = i == 0
    last = i == nsteps - 1

    @pl.when(first | (i == 1) | last)
    def _():
        ck = lax.broadcasted_iota(jnp.int32, (GRID_W, LANES), 0)
        lane = lax.broadcasted_iota(jnp.int32, (GRID_W, LANES), 1)
        cq = lane % GRID_W
        cs = jnp.clip(cq - NA_KW // 2, 0, GRID_W - NA_KW)
        col_ok = (ck >= cs) & (ck < cs + NA_KW)
        lo = lane < GRID_W

        def build(idx, carry):
            head = idx // NA_WIN_ROWS
            w = idx % NA_WIN_ROWS
            for pp in range(NA_ROWS_PER_STEP // 2):
                halves = []
                for half in range(2):
                    rl = 2 * pp + half
                    dy = w - rl + (NA_KH - 1 - NA_ROWS_PER_STEP)
                    p = jnp.broadcast_to(rpb_ref[pl.ds(head * n_dy + dy, 1), :], (GRID_W, LANES))
                    shift = (LANES - (NA_KW - 1) + half * GRID_W) % LANES
                    t = pltpu.roll(p, shift, axis=1, stride=1, stride_axis=0)
                    row_ok = jnp.where(first, w >= NA_ROWS_PER_STEP,
                                       jnp.where(last, w < NA_KH, (w >= rl) & (w < rl + NA_KH)))
                    halves.append((t, (lane * 0 + row_ok.astype(jnp.int32)) > 0))
                val = jnp.where(lo, halves[0][0], halves[1][0])
                ok = col_ok & ((lo & halves[0][1]) | (jnp.logical_not(lo) & halves[1][1]))
                tbl_ref[head, pl.ds(pl.multiple_of(w * GRID_W, GRID_W), GRID_W),
                        pp * LANES:(pp + 1) * LANES] = jnp.where(ok, val, -jnp.inf)
            return carry

        lax.fori_loop(0, NA_HEADS * NA_WIN_ROWS, build, 0)

    nt = (((1,), (1,)), ((), ()))
    lane_lo = lax.broadcasted_iota(jnp.int32, (1, LANES), 1) < NA_HEAD_DIM
    for p in range(NA_HEADS // 2):
        cols = slice(p * LANES, (p + 1) * LANES)
        qp = q_ref[:, cols]
        o_t = []
        for e in range(2):
            head = 2 * p + e
            sel = lane_lo if e == 0 else jnp.logical_not(lane_lo)
            qm = jnp.where(sel, qp, jnp.zeros_like(qp))
            s = [lax.dot_general(k_ref[:, cols], qm, nt, preferred_element_type=jnp.float32)
                 + tbl_ref[head, b * blk:(b + 1) * blk, :]
                 for b, k_ref in enumerate((kp_ref, kc_ref, kn_ref))]
            m = jnp.max(jnp.maximum(jnp.maximum(s[0], s[1]), s[2]), axis=0, keepdims=True)
            pe = [jnp.exp(sb - m) for sb in s]
            l = jnp.sum(pe[0] + pe[1] + pe[2], axis=0, keepdims=True)
            acc = None
            for b, v_ref in enumerate((vp_ref, vc_ref, vn_ref)):
                vt = v_ref[head * NA_HEAD_DIM:(head + 1) * NA_HEAD_DIM, :]
                c = jnp.dot(vt, pe[b].astype(vt.dtype), preferred_element_type=jnp.float32)
                acc = c if acc is None else acc + c
            o_t.append(acc * (1.0 / l))
        o = jnp.concatenate(o_t, axis=0).T
        z = z_ref[:, cols].astype(jnp.float32)
        o_ref[:, cols] = (o * _silu(z)).astype(o_ref.dtype)


def _na_call(rpb_rows, a, vt):
    t = a.shape[0]
    blk = NA_ROWS_PER_STEP * GRID_W
    nblk = t // blk
    assert nblk >= 3 and NA_KH == 2 * NA_ROWS_PER_STEP
    prev = lambda i: jnp.maximum(i - 1, 0)
    nxt = lambda i: jnp.minimum(i + 1, nblk - 1)
    tok = lambda f, c: pl.BlockSpec((blk, BW), lambda i: (f(i), c))
    val = lambda f: pl.BlockSpec((NA_W, blk), lambda i: (0, f(i)))
    cur = lambda i: i
    return pl.pallas_call(
        _na_kernel,
        grid=(nblk,),
        in_specs=[pl.BlockSpec(rpb_rows.shape, lambda i: (0, 0)),
                  tok(cur, COL_NA_Q),
                  tok(prev, COL_NA_K), tok(cur, COL_NA_K), tok(nxt, COL_NA_K),
                  val(prev), val(cur), val(nxt),
                  tok(cur, COL_NA_Z)],
        out_specs=pl.BlockSpec((blk, BW), lambda i: (i, 0)),
        out_shape=jax.ShapeDtypeStruct((t, BW), jnp.bfloat16),
        scratch_shapes=[pltpu.VMEM((NA_HEADS, NA_WIN_ROWS * GRID_W, blk), jnp.float32)],
        compiler_params=pltpu.CompilerParams(dimension_semantics=("arbitrary",),
                                             vmem_limit_bytes=VMEM_LIMIT),
        name="na_attn",
    )(rpb_rows, a, a, a, a, vt, vt, vt, a)


def _mlstm_kernel(qf_ref, kf_ref, vf_ref, grf_ref, gcf_ref, qb_ref, kb_ref, vb_ref, grb_ref, gcb_ref,
                  hf_ref, hb_ref, ct_ref, mu_ref):
    j = pl.program_id(0)
    lc = ML_CHUNK
    hd = ML_HEAD_DIM

    @pl.when(j == 0)
    def _():
        ct_ref[...] = jnp.zeros_like(ct_ref)
        mu_ref[...] = jnp.zeros_like(mu_ref)

    t_idx = lax.broadcasted_iota(jnp.int32, (lc, lc), 0)
    s_idx = lax.broadcasted_iota(jnp.int32, (lc, lc), 1)
    ones_col = (lax.broadcasted_iota(jnp.int32, (lc, hd), 1) == 0).astype(jnp.bfloat16)

    dirs = ((qf_ref, kf_ref, vf_ref, grf_ref, gcf_ref, hf_ref, s_idx <= t_idx, lc - 1, 0),
            (qb_ref, kb_ref, vb_ref, grb_ref, gcb_ref, hb_ref, s_idx >= t_idx, 0, G_DIR))
    for dirn, (q_ref, k_ref, v_ref, gr_ref, gc_ref, h_ref, mask, last, base) in enumerate(dirs):
        for hh in range(ML_HEADS):
            u = dirn * ML_HEADS + hh
            cols = slice(hh * hd, (hh + 1) * hd)
            q = q_ref[:, cols]
            k = k_ref[:, cols]
            v = v_ref[:, cols]
            g_row = gr_ref[base + G_G + hh:base + G_G + hh + 1, :]
            b_col = gc_ref[:, base + G_B + hh:base + G_B + hh + 1]
            g_col = gc_ref[:, base + G_G + hh:base + G_G + hh + 1]
            cm_col = gc_ref[:, base + G_CM + hh:base + G_CM + hh + 1]
            b_end = gc_ref[last:last + 1, base + G_B + hh:base + G_B + hh + 1]
            cm_end = gc_ref[last:last + 1, base + G_CM + hh:base + G_CM + hh + 1]
            mu = mu_ref[u][0:1, 0:1]

            m_run = jnp.maximum(mu, cm_col)
            m_end = jnp.maximum(mu, cm_end)
            pw = jnp.exp(jnp.where(mask, g_row - m_run, -jnp.inf))
            s = lax.dot_general(q, k, (((1,), (1,)), ((), ())), preferred_element_type=jnp.float32)
            a = (s * pw).astype(jnp.bfloat16)
            vext = jnp.concatenate([v, ones_col], axis=1)
            r1 = jnp.dot(a, vext, preferred_element_type=jnp.float32)
            ct = ct_ref[u]
            r2 = jnp.dot(q, ct.astype(jnp.bfloat16), preferred_element_type=jnp.float32)
            sc = jnp.exp(mu - m_run)
            num = r1[:, 0:hd] + sc * r2[:, 0:hd]
            den = r1[:, hd:hd + 1] + sc * r2[:, hd:hd + 1]
            bound = jnp.exp(-(b_col + m_run))
            h_ref[:, cols] = (num * (1.0 / jnp.maximum(jnp.abs(den), bound))).astype(h_ref.dtype)

            w_col = jnp.exp(g_col - m_end)
            wv = (w_col * vext.astype(jnp.float32)).astype(jnp.bfloat16)
            upd = lax.dot_general(k, wv, (((0,), (0,)), ((), ())), preferred_element_type=jnp.float32)
            ct_ref[u] = jnp.exp(mu - m_end) * ct + upd
            mu_ref[u] = jnp.broadcast_to(b_end + m_end, mu_ref.shape[1:])


def _mlstm_call(a, grow, gcol):
    t = a.shape[0]
    lc = ML_CHUNK
    nc = t // lc
    fwd = lambda c: (lambda j: (j, c))
    bwd = lambda c: (lambda j: (nc - 1 - j, c))
    tok = lambda im: pl.BlockSpec((lc, BW), im)
    return pl.pallas_call(
        _mlstm_kernel,
        grid=(nc,),
        in_specs=[tok(fwd(COL_ML_Q)), tok(fwd(COL_ML_K)), tok(fwd(COL_ML_V)),
                  pl.BlockSpec((G_ROWS, lc), lambda j: (0, j)),
                  pl.BlockSpec((lc, G_ROWS), lambda j: (j, 0)),
                  tok(bwd(COL_ML_Q)), tok(bwd(COL_ML_K)), tok(bwd(COL_ML_V)),
                  pl.BlockSpec((G_ROWS, lc), lambda j: (0, nc - 1 - j)),
                  pl.BlockSpec((lc, G_ROWS), lambda j: (nc - 1 - j, 0))],
        out_specs=[pl.BlockSpec((lc, BW), lambda j: (j, 0)),
                   pl.BlockSpec((lc, BW), lambda j: (nc - 1 - j, 0))],
        out_shape=[jax.ShapeDtypeStruct((t, BW), jnp.float32),
                   jax.ShapeDtypeStruct((t, BW), jnp.float32)],
        scratch_shapes=[pltpu.VMEM((2 * ML_HEADS, ML_HEAD_DIM, 2 * ML_HEAD_DIM), jnp.float32),
                        pltpu.VMEM((2 * ML_HEADS, 8, LANES), jnp.float32)],
        compiler_params=pltpu.CompilerParams(dimension_semantics=("arbitrary",),
                                             vmem_limit_bytes=VMEM_LIMIT),
        name="mlstm",
    )(a, a, a, grow, gcol, a, a, a, grow, gcol)


def _outproj_kernel(na_ref, hf_ref, hb_ref, o_ref, z_ref, mlw_ref, wout_ref, x_ref, mod_ref, fnw_ref, out_ref):
    d = x_ref.shape[1]
    hd = ML_HEAD_DIM
    hm = (hf_ref[...] + hb_ref[...]) * jax.nn.sigmoid(o_ref[...].astype(jnp.float32))
    parts = []
    for hh in range(ML_HEADS):
        seg = hm[:, hh * hd:(hh + 1) * hd]
        dev = seg - jnp.mean(seg, axis=-1, keepdims=True)
        var = jnp.mean(dev * dev, axis=-1, keepdims=True)
        parts.append(dev * lax.rsqrt(var + EPS))
    ml = jnp.concatenate(parts, axis=1) * mlw_ref[...] * _silu(z_ref[...].astype(jnp.float32))
    y = jnp.dot(na_ref[...], wout_ref[0:NA_W, :], preferred_element_type=jnp.float32)
    y = y + jnp.dot(ml.astype(jnp.bfloat16), wout_ref[NA_W:NA_W + ML_W, :], preferred_element_type=jnp.float32)
    res = x_ref[...] + mod_ref[:, 2 * d:3 * d] * y
    ms = jnp.mean(res * res, axis=-1, keepdims=True)
    out_ref[...] = (res * lax.rsqrt(ms + EPS) * fnw_ref[...]).astype(out_ref.dtype)


def _outproj_call(na_out, hf, hb, a, ml_norm_w, w_out, x, mod, fnw, tm):
    t, d = x.shape
    const = lambda i: (0, 0)
    return pl.pallas_call(
        _outproj_kernel,
        grid=(t // tm,),
        in_specs=[pl.BlockSpec((tm, BW), lambda i: (i, 0)),
                  pl.BlockSpec((tm, BW), lambda i: (i, 0)),
                  pl.BlockSpec((tm, BW), lambda i: (i, 0)),
                  pl.BlockSpec((tm, BW), lambda i: (i, COL_ML_O)),
                  pl.BlockSpec((tm, BW), lambda i: (i, COL_ML_Z)),
                  pl.BlockSpec((1, ML_W), const),
                  pl.BlockSpec(w_out.shape, const),
                  pl.BlockSpec((tm, d), lambda i: (i, 0)),
                  pl.BlockSpec(mod.shape, const),
                  pl.BlockSpec((1, d), const)],
        out_specs=pl.BlockSpec((tm, d), lambda i: (i, 0)),
        out_shape=jax.ShapeDtypeStruct((t, d), x.dtype),
        compiler_params=pltpu.CompilerParams(dimension_semantics=("arbitrary",),
                                             vmem_limit_bytes=VMEM_LIMIT),
        name="out_proj",
    )(na_out, hf, hb, a, a, ml_norm_w, w_out, x, mod, fnw)


def _layer(x, c_col, w_ada, b_ada, norm_w, w_in, b_in, conv_w, conv_b, rpb, ml_norm_w, w_out, fnw, tm):
    d = x.shape[1]
    bf = jnp.bfloat16
    mod = _ada_call(c_col, w_ada, b_ada[None, :])

    v0, v1 = 2 * NA_W, 3 * NA_W
    q0, q1, g0 = 4 * NA_W, 4 * NA_W + 2 * ML_W, 4 * NA_W + 5 * ML_W
    w_main = jnp.concatenate([w_in[:, :v0], w_in[:, v1:q0], w_in[:, q1:g0]], axis=1).astype(bf)
    b_main = jnp.concatenate([b_in[:v0], b_in[v1:q0], b_in[q1:g0]])[None, :]
    w_qk = w_in[:, q0:q1].astype(bf)
    b_qk = b_in[q0:q1][None, :]
    w_vt = w_in[:, v0:v1].T.astype(bf)
    b_vt = b_in[v0:v1][:, None]
    nh = ML_HEADS
    perm = jnp.array(list(range(0, nh)) + list(range(2 * nh, 3 * nh)) + list(range(nh, 2 * nh))
                     + list(range(3 * nh, 4 * nh)), jnp.int32)
    w_g = w_in[:, g0:].T[perm].astype(bf)
    b_g = b_in[g0:][perm][:, None]

    a, vt, grow, gcol = _inproj_call(x, mod, norm_w[None, :], w_main, w_qk, w_vt, b_main, b_qk, b_vt, w_g, b_g,
                                     conv_w, conv_b[None, :], tm)

    n_dy, n_dx = 2 * NA_KH - 1, 2 * NA_KW - 1
    rpb_rows = jnp.pad(rpb[:, :, ::-1].reshape(NA_HEADS * n_dy, n_dx), ((0, 0), (0, LANES - n_dx)))
    na_out = _na_call(rpb_rows, a, vt)
    hf, hb = _mlstm_call(a, grow, gcol)
    return _outproj_call(na_out, hf, hb, a, ml_norm_w[None, :], w_out.astype(bf), x, mod, fnw[None, :], tm)


def kernel(x, c, w_ada, b_ada, norm_w, w_in, b_in, conv_w, conv_b, rpb, ml_norm_w, w_out, final_norm_w):
    batch, t, d = x.shape
    depth = w_ada.shape[0]
    assert batch == 1 and depth == 1, "one sequence, one layer"
    assert t % (NA_ROWS_PER_STEP * GRID_W) == 0 and t % ML_CHUNK == 0
    tm = 512 if t % 512 == 0 else ML_CHUNK
    out = _layer(x[0].astype(jnp.float32), c.astype(jnp.float32).reshape(d, 1), w_ada[0], b_ada[0], norm_w[0],
                 w_in[0], b_in[0], conv_w[0], conv_b[0], rpb[0], ml_norm_w[0], w_out[0], final_norm_w, tm)
    return out[None].astype(x.dtype)
```
